```python
import jax, jax.numpy as jnp
from jax import lax
import numpy as np

D_MODEL = 4096
BATCH = 4
SEQ = 4096
DEPTH = 4

N_BRANCH = 4
W_MIX = D_MODEL // 4
CONV_A_WIDTH = 31
CONV_B_WIDTH = 3
POOL_WINDOWS = (2, 4, 8, 16)
N_POOL_GROUPS = 4
POOL_GROUP = W_MIX // N_POOL_GROUPS
SGU_CHUNK = 128
N_SGU_GROUPS = 4
SGU_GROUP = W_MIX // N_SGU_GROUPS
GATE_RANK = D_MODEL // 8
D_FF = 4 * D_MODEL
OFF_A = 0
OFF_B = OFF_A + 2 * W_MIX
OFF_C = OFF_B + 3 * W_MIX
OFF_D = OFF_C + W_MIX
OFF_G = OFF_D + 2 * W_MIX
IN_COLS = OFF_G + GATE_RANK
LN_EPS = 1e-5
DEEPNORM_ALPHA = (2 * DEPTH) ** 0.25
DEEPNORM_BETA = (8 * DEPTH) ** -0.25

kernel_name = "hybrid_gated_conv_pool_sgu_deepnorm"


def layer_norm(x, g, b):
    xf = x.astype(jnp.float32)
    mu = jnp.mean(xf, axis=-1, keepdims=True)
    var = jnp.mean(jnp.square(xf - mu), axis=-1, keepdims=True)
    y = (xf - mu) * lax.rsqrt(var + LN_EPS)
    return (y * g.astype(jnp.float32) + b.astype(jnp.float32)).astype(x.dtype)


def causal_depthwise_conv(x, w):
    k = w.shape[0]
    return lax.conv_general_dilated(
        x, w[:, None, :].astype(x.dtype), window_strides=(1,), padding=[(k - 1, 0)],
        dimension_numbers=("NWC", "WIO", "NWC"), feature_group_count=x.shape[-1])


def mixer_conformer_conv(z, conv_w, conv_b, ln_g, ln_b):
    val, gate = jnp.split(z, 2, axis=-1)
    h = val * jax.nn.sigmoid(gate)
    h = causal_depthwise_conv(h, conv_w) + conv_b
    h = layer_norm(h, ln_g, ln_b)
    return jax.nn.silu(h)


def mixer_short_conv(z, conv_w):
    bg, cg, h = jnp.split(z, 3, axis=-1)
    return bg * causal_depthwise_conv(cg * h, conv_w)


def mixer_multiscale_pool(z, pool_w, pool_scale):
    s = z.shape[1]
    zf = z.astype(jnp.float32)
    cs = jnp.cumsum(zf, axis=1)
    pos = jnp.arange(1, s + 1, dtype=jnp.float32)[:, None]
    outs = []
    for g, w in enumerate(POOL_WINDOWS):
        sl = slice(g * POOL_GROUP, (g + 1) * POOL_GROUP)
        csg = cs[..., sl]
        prev = jnp.pad(csg, ((0, 0), (w, 0), (0, 0)))[:, :s]
        mean = (csg - prev) / jnp.minimum(pos, float(w))
        pooled = (mean - zf[..., sl]).astype(z.dtype)
        outs.append(jnp.einsum("bsc,cd->bsd", pooled, pool_w[g]))
    return jnp.concatenate(outs, axis=-1) * pool_scale


def mixer_chunked_sgu(z, ln_g, ln_b, sgu_w, sgu_b):
    z = jax.nn.gelu(z)
    u, v = jnp.split(z, 2, axis=-1)
    v = layer_norm(v, ln_g, ln_b)
    bsz, s, _ = v.shape
    n_chunks = s // SGU_CHUNK
    v = v.reshape(bsz, n_chunks, SGU_CHUNK, N_SGU_GROUPS, SGU_GROUP)
    mask = jnp.tril(jnp.ones((SGU_CHUNK, SGU_CHUNK), dtype=bool))
    w = jnp.where(mask[None], sgu_w, 0)
    sp = jnp.einsum("gts,bnsgc->bntgc", w, v) + sgu_b.T[:, :, None]
    return u * sp.reshape(bsz, s, W_MIX)


def setup_inputs(seed: int = 0) -> dict:
    key = jax.random.key(seed)
    ks = jax.random.split(key, 24)
    f32 = jnp.float32
    nrm = lambda k, shape, scale: jax.random.normal(k, shape, f32) * scale
    L, D = DEPTH, D_MODEL
    return {
        "x": nrm(ks[0], (BATCH, SEQ, D), 1.0),
        "w_in": nrm(ks[1], (L, D, IN_COLS), D ** -0.5),
        "conv_a_w": nrm(ks[2], (L, CONV_A_WIDTH, W_MIX), CONV_A_WIDTH ** -0.5),
        "conv_a_b": nrm(ks[3], (L, W_MIX), 0.02),
        "ln_a_g": 1.0 + nrm(ks[4], (L, W_MIX), 0.02),
        "ln_a_b": nrm(ks[5], (L, W_MIX), 0.02),
        "conv_b_w": nrm(ks[6], (L, CONV_B_WIDTH, W_MIX), CONV_B_WIDTH ** -0.5),
        "pool_w": nrm(ks[7], (L, N_POOL_GROUPS, POOL_GROUP, POOL_GROUP), POOL_GROUP ** -0.5),
        "pool_scale": 1.0 + nrm(ks[8], (L, W_MIX), 0.02),
        "ln_d_g": 1.0 + nrm(ks[9], (L, W_MIX), 0.02),
        "ln_d_b": nrm(ks[10], (L, W_MIX), 0.02),
        "sgu_w": nrm(ks[11], (L, N_SGU_GROUPS, SGU_CHUNK, SGU_CHUNK), SGU_CHUNK ** -0.5),
        "sgu_b": 1.0 + nrm(ks[12], (L, N_SGU_GROUPS, SGU_CHUNK), 0.02),
        "w_branch": nrm(ks[13], (L, N_BRANCH, W_MIX, D), W_MIX ** -0.5),
        "w_gate": nrm(ks[14], (L, N_BRANCH, GATE_RANK, D), GATE_RANK ** -0.5),
        "b_gate": nrm(ks[15], (L, N_BRANCH, D), 0.02),
        "w_o": nrm(ks[16], (L, D, D), DEEPNORM_BETA * D ** -0.5),
        "ln1_g": 1.0 + nrm(ks[17], (L, D), 0.02),
        "ln1_b": nrm(ks[18], (L, D), 0.02),
        "w_up": nrm(ks[19], (L, D, D_FF), D ** -0.5),
        "w_down": nrm(ks[20], (L, D_FF, D), DEEPNORM_BETA * D_FF ** -0.5),
        "ln2_g": 1.0 + nrm(ks[21], (L, D), 0.02),
        "ln2_b": nrm(ks[22], (L, D), 0.02),
    }


def reference(x, w_in, conv_a_w, conv_a_b, ln_a_g, ln_a_b, conv_b_w, pool_w, pool_scale,
              ln_d_g, ln_d_b, sgu_w, sgu_b, w_branch, w_gate, b_gate, w_o, ln1_g, ln1_b,
              w_up, w_down, ln2_g, ln2_b):
    for l in range(DEPTH):
        z = x @ w_in[l]
        zg = z[..., OFF_G:]
        branches = (
            mixer_conformer_conv(z[..., OFF_A:OFF_B], conv_a_w[l], conv_a_b[l], ln_a_g[l], ln_a_b[l]),
            mixer_short_conv(z[..., OFF_B:OFF_C], conv_b_w[l]),
            mixer_multiscale_pool(z[..., OFF_C:OFF_D], pool_w[l], pool_scale[l]),
            mixer_chunked_sgu(z[..., OFF_D:OFF_G], ln_d_g[l], ln_d_b[l], sgu_w[l], sgu_b[l]),
        )
        merged = jnp.zeros_like(x)
        for i, y in enumerate(branches):
            gate = jax.nn.sigmoid(zg @ w_gate[l, i] + b_gate[l, i])
            merged = merged + gate * (y @ w_branch[l, i])
        x = layer_norm(DEEPNORM_ALPHA * x + merged @ w_o[l], ln1_g[l], ln1_b[l])
        h = jnp.square(jax.nn.relu(x @ w_up[l]))
        x = layer_norm(DEEPNORM_ALPHA * x + h @ w_down[l], ln2_g[l], ln2_b[l])
    return x
```

```python
import functools

import jax
import jax.numpy as jnp
from jax import lax
from jax.experimental import pallas as pl
from jax.experimental.pallas import tpu as pltpu

F32 = jnp.float32
BF16 = jnp.bfloat16

LN_EPS = 1e-5
POOL_WINDOWS = (2, 4, 8, 16)

V7X_VMEM_BYTES = 64 * 1024 * 1024
V7X_LANES = 128
V7X_SUBLANES = 8
V7X_MXU_DIM = 256

CONV_ROW_CHUNK = 64


def _round_up(n, m):
    return (n + m - 1) // m * m


def _tile(dim, pref, align):
    if dim <= pref:
        return dim
    t = pref - pref % align
    while t >= align:
        if dim % t == 0:
            return t
        t -= align
    return dim


def _params(semantics, block_bytes, scratch_bytes=0, temp_bytes=0):
    need = 2 * block_bytes + scratch_bytes + temp_bytes
    limit = min(max(need, 16 * 1024 * 1024), V7X_VMEM_BYTES - 6 * 1024 * 1024)
    return pltpu.CompilerParams(dimension_semantics=semantics, vmem_limit_bytes=limit)


def _nbytes(shape, dtype):
    n = 1
    for s in shape:
        n *= s
    return n * jnp.dtype(dtype).itemsize


def _matmul_kernel(x_ref, w_ref, o_ref, *, relu2):
    acc = jnp.dot(x_ref[...], w_ref[...], preferred_element_type=F32)
    if relu2:
        r = jnp.maximum(acc, 0.0)
        acc = r * r
    o_ref[...] = acc.astype(o_ref.dtype)


def _matmul(x, w, out_dtype, *, relu2=False, tm=1024, tn=1024, name):
    m, k = x.shape
    n = w.shape[1]
    tm = _tile(m, tm, V7X_SUBLANES * 2)
    tn = _tile(n, tn, V7X_MXU_DIM)
    blocks = _nbytes((tm, k), x.dtype) + _nbytes((k, tn), w.dtype) + _nbytes((tm, tn), out_dtype)
    return pl.pallas_call(
        functools.partial(_matmul_kernel, relu2=relu2),
        grid=(m // tm, n // tn),
        in_specs=[
            pl.BlockSpec((tm, k), lambda i, j: (i, 0)),
            pl.BlockSpec((k, tn), lambda i, j: (0, j)),
        ],
        out_specs=pl.BlockSpec((tm, tn), lambda i, j: (i, j)),
        out_shape=jax.ShapeDtypeStruct((m, n), out_dtype),
        compiler_params=_params(("parallel", "parallel"), blocks, temp_bytes=2 * _nbytes((tm, tn), F32)),
        name=name,
    )(x, w)


def _matmul_residual_kernel(x_ref, w_ref, r_ref, o_ref, *, alpha):
    @pl.when(pl.program_id(2) == 0)
    def _():
        o_ref[...] = alpha * r_ref[...]

    o_ref[...] += jnp.dot(x_ref[...], w_ref[...], preferred_element_type=F32)


def _matmul_residual(x, w, resid, alpha, *, tm=1024, tn=1024, tk=4096, name):
    m, k = x.shape
    n = w.shape[1]
    tm = _tile(m, tm, V7X_SUBLANES * 2)
    tn = _tile(n, tn, V7X_MXU_DIM)
    tk = _tile(k, tk, V7X_MXU_DIM)
    blocks = (_nbytes((tm, tk), x.dtype) + _nbytes((tk, tn), w.dtype)
              + 2 * _nbytes((tm, tn), F32))
    return pl.pallas_call(
        functools.partial(_matmul_residual_kernel, alpha=alpha),
        grid=(m // tm, n // tn, k // tk),
        in_specs=[
            pl.BlockSpec((tm, tk), lambda i, j, kk: (i, kk)),
            pl.BlockSpec((tk, tn), lambda i, j, kk: (kk, j)),
            pl.BlockSpec((tm, tn), lambda i, j, kk: (i, j)),
        ],
        out_specs=pl.BlockSpec((tm, tn), lambda i, j, kk: (i, j)),
        out_shape=jax.ShapeDtypeStruct((m, n), F32),
        compiler_params=_params(("parallel", "parallel", "arbitrary"), blocks,
                                temp_bytes=2 * _nbytes((tm, tn), F32)),
        name=name,
    )(x, w, resid)


def _ln(v, g, b):
    mu = jnp.mean(v, axis=-1, keepdims=True)
    d = v - mu
    var = jnp.mean(d * d, axis=-1, keepdims=True)
    return d * lax.rsqrt(var + LN_EPS) * g + b


def _layer_norm_kernel(x_ref, g_ref, b_ref, o_ref, ob_ref):
    y = _ln(x_ref[...], g_ref[...], b_ref[...])
    o_ref[...] = y
    ob_ref[...] = y.astype(BF16)


def _layer_norm(x, g, b, *, tr=256, name):
    m, d = x.shape
    tr = _tile(m, tr, V7X_SUBLANES * 2)
    blocks = 2 * _nbytes((tr, d), F32) + _nbytes((tr, d), BF16)
    return pl.pallas_call(
        _layer_norm_kernel,
        grid=(m // tr,),
        in_specs=[
            pl.BlockSpec((tr, d), lambda i: (i, 0)),
            pl.BlockSpec((1, d), lambda i: (0, 0)),
            pl.BlockSpec((1, d), lambda i: (0, 0)),
        ],
        out_specs=[
            pl.BlockSpec((tr, d), lambda i: (i, 0)),
            pl.BlockSpec((tr, d), lambda i: (i, 0)),
        ],
        out_shape=[jax.ShapeDtypeStruct((m, d), F32), jax.ShapeDtypeStruct((m, d), BF16)],
        compiler_params=_params(("parallel",), blocks, temp_bytes=2 * _nbytes((tr, d), F32)),
        name=name,
    )(x, g.reshape(1, d), b.reshape(1, d))


def _merge_kernel(y_ref, zg_ref, wb_ref, wg_ref, bg_ref, o_ref, *, n_branch, w_mix):
    zg = zg_ref[...]
    acc = None
    for i in range(n_branch):
        gate = jax.nn.sigmoid(jnp.dot(zg, wg_ref[i], preferred_element_type=F32) + bg_ref[i])
        br = jnp.dot(y_ref[:, i * w_mix:(i + 1) * w_mix], wb_ref[i], preferred_element_type=F32)
        acc = gate * br if acc is None else acc + gate * br
    o_ref[...] = acc.astype(o_ref.dtype)


def _merge(y, zg, w_branch, w_gate, b_gate, *, tm=1024, tn=512, name):
    m = y.shape[0]
    n_branch, w_mix, d = w_branch.shape
    rank = w_gate.shape[1]
    tm = _tile(m, tm, V7X_SUBLANES * 2)
    tn = _tile(d, tn, V7X_MXU_DIM)
    blocks = (_nbytes((tm, n_branch * w_mix), BF16) + _nbytes((tm, rank), BF16)
              + _nbytes((n_branch, w_mix, tn), BF16) + _nbytes((n_branch, rank, tn), BF16)
              + _nbytes((n_branch, 1, tn), F32) + _nbytes((tm, tn), BF16))
    return pl.pallas_call(
        functools.partial(_merge_kernel, n_branch=n_branch, w_mix=w_mix),
        grid=(m // tm, d // tn),
        in_specs=[
            pl.BlockSpec((tm, n_branch * w_mix), lambda i, j: (i, 0)),
            pl.BlockSpec((tm, rank), lambda i, j: (i, 0)),
            pl.BlockSpec((n_branch, w_mix, tn), lambda i, j: (0, 0, j)),
            pl.BlockSpec((n_branch, rank, tn), lambda i, j: (0, 0, j)),
            pl.BlockSpec((n_branch, 1, tn), lambda i, j: (0, 0, j)),
        ],
        out_specs=pl.BlockSpec((tm, tn), lambda i, j: (i, j)),
        out_shape=jax.ShapeDtypeStruct((m, d), BF16),
        compiler_params=_params(("parallel", "parallel"), blocks, temp_bytes=6 * _nbytes((tm, tn), F32)),
        name=name,
    )(y, zg, w_branch, w_gate, b_gate.reshape(n_branch, 1, d))


def _mixers_kernel(val_ref, gate_ref, bg_ref, cg_ref, hh_ref, pc_ref, u_ref, v_ref, zg_ref,
                   valh_ref, gateh_ref, cgh_ref, hhh_ref, pch_ref,
                   caw_ref, cab_ref, lag_ref, lab_ref, cbw_ref, pw_ref, ps_ref,
                   ldg_ref, ldb_ref, sw_ref, sbt_ref,
                   y_ref, zgb_ref,
                   hext_ref, gext_ref, pext_ref, conv_ref,
                   *, ts, seq_tiles, w_mix, halo_a, halo_b, halo_c):
    i = pl.program_id(0)
    seq_tile = i % seq_tiles
    first = seq_tile == 0

    ka = caw_ref.shape[0]
    hext_ref[0:halo_a, :] = jnp.where(first, 0.0, valh_ref[...] * jax.nn.sigmoid(gateh_ref[...]))
    hext_ref[halo_a:halo_a + ts, :] = val_ref[...] * jax.nn.sigmoid(gate_ref[...])
    rc = min(CONV_ROW_CHUNK, ts)
    for c0 in range(0, w_mix, V7X_LANES):
        cols = slice(c0, c0 + V7X_LANES)
        wk = caw_ref[:, cols]
        bias = cab_ref[:, cols]
        for r0 in range(0, ts, rc):
            acc = jnp.broadcast_to(bias, (rc, V7X_LANES))
            for k in range(ka):
                start = r0 + halo_a - (ka - 1) + k
                acc = acc + wk[k:k + 1, :] * hext_ref[start:start + rc, cols]
            conv_ref[r0:r0 + rc, cols] = acc
    ya = _ln(conv_ref[...], lag_ref[...], lab_ref[...])
    y_ref[:, 0:w_mix] = (ya * jax.nn.sigmoid(ya)).astype(BF16)

    kb = cbw_ref.shape[0]
    gext_ref[0:halo_b, :] = jnp.where(first, 0.0, cgh_ref[...] * hhh_ref[...])
    gext_ref[halo_b:halo_b + ts, :] = cg_ref[...] * hh_ref[...]
    cb = None
    for k in range(kb):
        start = halo_b - (kb - 1) + k
        term = cbw_ref[k:k + 1, :] * gext_ref[start:start + ts, :]
        cb = term if cb is None else cb + term
    y_ref[:, w_mix:2 * w_mix] = (bg_ref[...] * cb).astype(BF16)

    n_groups = pw_ref.shape[0]
    pg = w_mix // n_groups
    pext_ref[0:halo_c, :] = jnp.where(first, 0.0, pch_ref[...])
    pext_ref[halo_c:halo_c + ts, :] = pc_ref[...]
    pos = (lax.broadcasted_iota(jnp.int32, (ts, pg), 0) + (seq_tile * ts + 1)).astype(F32)
    for g, win in enumerate(POOL_WINDOWS):
        cols = slice(g * pg, (g + 1) * pg)
        tot = None
        for j in range(win):
            term = pext_ref[halo_c - j:halo_c - j + ts, cols]
            tot = term if tot is None else tot + term
        mean = tot / jnp.minimum(pos, float(win))
        pooled = (mean - pc_ref[:, cols]).astype(BF16)
        out = jnp.dot(pooled, pw_ref[g], preferred_element_type=F32)
        y_ref[:, 2 * w_mix + g * pg:2 * w_mix + (g + 1) * pg] = (out * ps_ref[:, cols]).astype(BF16)

    chunk = sw_ref.shape[1]
    sg = w_mix // sw_ref.shape[0]
    uu = jax.nn.gelu(u_ref[...])
    vn = _ln(jax.nn.gelu(v_ref[...]), ldg_ref[...], ldb_ref[...]).astype(BF16)
    causal = (lax.broadcasted_iota(jnp.int32, (chunk, chunk), 0)
              >= lax.broadcasted_iota(jnp.int32, (chunk, chunk), 1))
    for g in range(sw_ref.shape[0]):
        wt = jnp.where(causal, sw_ref[g], 0.0).astype(BF16)
        bias = sbt_ref[:, g:g + 1]
        for c in range(ts // chunk):
            rows = slice(c * chunk, (c + 1) * chunk)
            cols = slice(g * sg, (g + 1) * sg)
            sp = jnp.dot(wt, vn[rows, cols], preferred_element_type=F32) + bias
            y_ref[rows, 3 * w_mix + g * sg:3 * w_mix + (g + 1) * sg] = (uu[rows, cols] * sp).astype(BF16)

    zgb_ref[...] = zg_ref[...].astype(BF16)


def _mixers(z, seq, conv_a_w, conv_a_b, ln_a_g, ln_a_b, conv_b_w, pool_w, pool_scale,
            ln_d_g, ln_d_b, sgu_w, sgu_b, rank, *, ts=256, name):
    m = z.shape[0]
    ka, w_mix = conv_a_w.shape
    kb = conv_b_w.shape[0]
    chunk = sgu_w.shape[-1]
    halo_a = _round_up(ka - 1, V7X_SUBLANES)
    halo_b = _round_up(kb - 1, V7X_SUBLANES)
    halo_c = _round_up(max(POOL_WINDOWS) - 1, V7X_SUBLANES)
    ts = _tile(seq, ts, max(chunk, halo_a, halo_c))
    assert ts % chunk == 0 and ts % halo_a == 0 and ts % halo_c == 0 and ts % halo_b == 0
    assert (8 * w_mix) % rank == 0
    seq_tiles = seq // ts

    def col(c):
        return pl.BlockSpec((ts, w_mix), lambda i: (i, c))

    def halo(rows, c):
        per = ts // rows
        return pl.BlockSpec((rows, w_mix), lambda i: (jnp.maximum(i * per - 1, 0), c))

    def whole(shape):
        return pl.BlockSpec(shape, lambda i: (0,) * len(shape))

    row = lambda a: a.reshape(1, w_mix)
    params = (conv_a_w, row(conv_a_b), row(ln_a_g), row(ln_a_b), conv_b_w, pool_w.astype(BF16),
              row(pool_scale), row(ln_d_g), row(ln_d_b), sgu_w, sgu_b.T)
    in_specs = ([col(c) for c in range(8)]
                + [pl.BlockSpec((ts, rank), lambda i: (i, 8 * w_mix // rank))]
                + [halo(halo_a, 0), halo(halo_a, 1), halo(halo_b, 3), halo(halo_b, 4), halo(halo_c, 5)]
                + [whole(p.shape) for p in params])
    blocks = (8 * _nbytes((ts, w_mix), F32) + _nbytes((ts, rank), F32)
              + 5 * _nbytes((halo_a, w_mix), F32) + sum(_nbytes(p.shape, p.dtype) for p in params)
              + _nbytes((ts, 4 * w_mix), BF16) + _nbytes((ts, rank), BF16))
    scratch = [
        pltpu.VMEM((halo_a + ts, w_mix), F32),
        pltpu.VMEM((halo_b + ts, w_mix), F32),
        pltpu.VMEM((halo_c + ts, w_mix), F32),
        pltpu.VMEM((ts, w_mix), F32),
    ]
    return pl.pallas_call(
        functools.partial(_mixers_kernel, ts=ts, seq_tiles=seq_tiles, w_mix=w_mix,
                          halo_a=halo_a, halo_b=halo_b, halo_c=halo_c),
        grid=(m // ts,),
        in_specs=in_specs,
        out_specs=[
            pl.BlockSpec((ts, 4 * w_mix), lambda i: (i, 0)),
            pl.BlockSpec((ts, rank), lambda i: (i, 0)),
        ],
        out_shape=[jax.ShapeDtypeStruct((m, 4 * w_mix), BF16), jax.ShapeDtypeStruct((m, rank), BF16)],
        scratch_shapes=scratch,
        compiler_params=_params(("parallel",), blocks,
                                scratch_bytes=4 * _nbytes((halo_a + ts, w_mix), F32),
                                temp_bytes=8 * _nbytes((ts, w_mix), F32)),
        name=name,
    )(*([z] * 14), *params)


def kernel(x, w_in, conv_a_w, conv_a_b, ln_a_g, ln_a_b, conv_b_w, pool_w, pool_scale, ln_d_g, ln_d_b,
           sgu_w, sgu_b, w_branch, w_gate, b_gate, w_o, ln1_g, ln1_b, w_up, w_down, ln2_g, ln2_b):
    batch, seq, d = x.shape
    depth = w_in.shape[0]
    rank = w_gate.shape[2]
    alpha = float((2 * depth) ** 0.25)
    m = batch * seq

    xf = x.reshape(m, d)
    xb = xf.astype(BF16)
    w_in_b, w_branch_b, w_gate_b = w_in.astype(BF16), w_branch.astype(BF16), w_gate.astype(BF16)
    w_o_b, w_up_b, w_down_b = w_o.astype(BF16), w_up.astype(BF16), w_down.astype(BF16)

    for l in range(depth):
        z = _matmul(xb, w_in_b[l], F32, tn=512, name=f"in_proj_{l}")
        y, zg = _mixers(z, seq, conv_a_w[l], conv_a_b[l], ln_a_g[l], ln_a_b[l], conv_b_w[l], pool_w[l],
                        pool_scale[l], ln_d_g[l], ln_d_b[l], sgu_w[l], sgu_b[l], rank, name=f"mixers_{l}")
        merged = _merge(y, zg, w_branch_b[l], w_gate_b[l], b_gate[l], name=f"merge_{l}")
        pre = _matmul_residual(merged, w_o_b[l], xf, alpha, name=f"out_proj_{l}")
        xf, xb = _layer_norm(pre, ln1_g[l], ln1_b[l], name=f"ln1_{l}")
        h = _matmul(xb, w_up_b[l], BF16, relu2=True, name=f"ffn_up_{l}")
        pre = _matmul_residual(h, w_down_b[l], xf, alpha, name=f"ffn_down_{l}")
        xf, xb = _layer_norm(pre, ln2_g[l], ln2_b[l], name=f"ln2_{l}")
    return xf.reshape(batch, seq, d)
```

```python
import functools
from typing import NamedTuple

import jax
import jax.numpy as jnp
from jax import lax
from jax.experimental import pallas as pl
from jax.experimental.pallas import tpu as pltpu

F32 = jnp.float32
BF16 = jnp.bfloat16

LN_EPS = 1e-5
POOL_WINDOWS = (2, 4, 8, 16)

V7X_VMEM_BYTES = 64 * 1024 * 1024
V7X_LANES = 128
V7X_SUBLANES = 8
V7X_BF16_ROWS = 16
V7X_MXU_DIM = 256

CONV_ROW_CHUNK = 64


def _round_up(n, m):
    return (n + m - 1) // m * m


def _tile(dim, pref, align):
    if dim <= pref:
        return dim
    t = pref - pref % align
    while t >= align:
        if dim % t == 0:
            return t
        t -= align
    return dim


def _nbytes(shape, dtype):
    n = 1
    for s in shape:
        n *= s
    return n * jnp.dtype(dtype).itemsize


def _params(semantics, block_bytes, scratch_bytes=0, temp_bytes=0):
    need = 2 * block_bytes + scratch_bytes + temp_bytes
    limit = min(max(need, 16 * 1024 * 1024), V7X_VMEM_BYTES - 6 * 1024 * 1024)
    return pltpu.CompilerParams(dimension_semantics=semantics, vmem_limit_bytes=limit)


class SideCast(NamedTuple):
    stacked: jax.Array
    layer: int


class _SidePlan(NamedTuple):
    in_specs: list
    out_specs: list
    out_shapes: list
    args: list
    block_bytes: int


def _plan_side_casts(casts, grid):
    steps = 1
    for g in grid:
        steps *= g

    def linear_step(*idx):
        s = idx[0]
        for g, i in zip(grid[1:], idx[1:]):
            s = s * g + i
        return s

    plan = _SidePlan([], [], [], [], 0)
    block_bytes = 0
    for cast in casts:
        _, rows, cols = cast.stacked.shape
        rb = rows
        for cand in range(V7X_BF16_ROWS, rows + 1, V7X_BF16_ROWS):
            if rows % cand == 0 and rows // cand <= steps:
                rb = cand
                break
        nb = rows // rb
        layer = cast.layer
        plan.in_specs.append(pl.BlockSpec(
            (None, rb, cols), lambda *idx, nb=nb, layer=layer: (layer, jnp.minimum(linear_step(*idx), nb - 1), 0)))
        plan.out_specs.append(pl.BlockSpec(
            (rb, cols), lambda *idx, nb=nb: (jnp.minimum(linear_step(*idx), nb - 1), 0)))
        plan.out_shapes.append(jax.ShapeDtypeStruct((rows, cols), BF16))
        plan.args.append(cast.stacked)
        block_bytes += _nbytes((rb, cols), F32) + _nbytes((rb, cols), BF16)
    return plan._replace(block_bytes=block_bytes)


def _run_side_casts(side_in_refs, side_out_refs):
    for src, dst in zip(side_in_refs, side_out_refs):
        dst[...] = src[...].astype(BF16)


def _matmul_kernel(*refs, relu2, n_side):
    x_ref, w_ref = refs[:2]
    side_in = refs[2:2 + n_side]
    o_ref = refs[2 + n_side]
    side_out = refs[3 + n_side:]
    acc = jnp.dot(x_ref[...], w_ref[...], preferred_element_type=F32)
    if relu2:
        r = jnp.maximum(acc, 0.0)
        acc = r * r
    o_ref[...] = acc.astype(o_ref.dtype)
    _run_side_casts(side_in, side_out)


def _matmul(x, w, out_dtype, *, relu2=False, tm=1024, tn=1024, casts=(), name):
    m, k = x.shape
    n = w.shape[1]
    tm = _tile(m, tm, V7X_BF16_ROWS)
    tn = _tile(n, tn, V7X_MXU_DIM)
    grid = (m // tm, n // tn)
    side = _plan_side_casts(casts, grid)
    blocks = (_nbytes((tm, k), x.dtype) + _nbytes((k, tn), w.dtype) + _nbytes((tm, tn), out_dtype)
              + side.block_bytes)
    return pl.pallas_call(
        functools.partial(_matmul_kernel, relu2=relu2, n_side=len(casts)),
        grid=grid,
        in_specs=[
            pl.BlockSpec((tm, k), lambda i, j: (i, 0)),
            pl.BlockSpec((k, tn), lambda i, j: (0, j)),
        ] + side.in_specs,
        out_specs=[pl.BlockSpec((tm, tn), lambda i, j: (i, j))] + side.out_specs,
        out_shape=[jax.ShapeDtypeStruct((m, n), out_dtype)] + side.out_shapes,
        compiler_params=_params(("arbitrary", "arbitrary"), blocks, temp_bytes=2 * _nbytes((tm, tn), F32)),
        name=name,
    )(x, w, *side.args)


def _matmul_residual_kernel(*refs, alpha, n_side):
    x_ref, w_ref, r_ref = refs[:3]
    side_in = refs[3:3 + n_side]
    o_ref = refs[3 + n_side]
    side_out = refs[4 + n_side:]

    @pl.when(pl.program_id(2) == 0)
    def _():
        o_ref[...] = alpha * r_ref[...]

    o_ref[...] += jnp.dot(x_ref[...], w_ref[...], preferred_element_type=F32)
    _run_side_casts(side_in, side_out)


def _matmul_residual(x, w, resid, alpha, *, tm=1024, tn=1024, tk=4096, casts=(), name):
    m, k = x.shape
    n = w.shape[1]
    tm = _tile(m, tm, V7X_BF16_ROWS)
    tn = _tile(n, tn, V7X_MXU_DIM)
    tk = _tile(k, tk, V7X_MXU_DIM)
    grid = (m // tm, n // tn, k // tk)
    side = _plan_side_casts(casts, grid)
    blocks = (_nbytes((tm, tk), x.dtype) + _nbytes((tk, tn), w.dtype)
              + 2 * _nbytes((tm, tn), F32) + side.block_bytes)
    return pl.pallas_call(
        functools.partial(_matmul_residual_kernel, alpha=alpha, n_side=len(casts)),
        grid=grid,
        in_specs=[
            pl.BlockSpec((tm, tk), lambda i, j, kk: (i, kk)),
            pl.BlockSpec((tk, tn), lambda i, j, kk: (kk, j)),
            pl.BlockSpec((tm, tn), lambda i, j, kk: (i, j)),
        ] + side.in_specs,
        out_specs=[pl.BlockSpec((tm, tn), lambda i, j, kk: (i, j))] + side.out_specs,
        out_shape=[jax.ShapeDtypeStruct((m, n), F32)] + side.out_shapes,
        compiler_params=_params(("arbitrary", "arbitrary", "arbitrary"), blocks,
                                temp_bytes=2 * _nbytes((tm, tn), F32)),
        name=name,
    )(x, w, resid, *side.args)


def _ln(v, g, b):
    mu = jnp.mean(v, axis=-1, keepdims=True)
    d = v - mu
    var = jnp.mean(d * d, axis=-1, keepdims=True)
    return d * lax.rsqrt(var + LN_EPS) * g + b


def _layer_norm_kernel(x_ref, g_ref, b_ref, o_ref, ob_ref):
    y = _ln(x_ref[...], g_ref[...], b_ref[...])
    o_ref[...] = y
    ob_ref[...] = y.astype(BF16)


def _layer_norm(x, g, b, *, tr=256, name):
    m, d = x.shape
    tr = _tile(m, tr, V7X_BF16_ROWS)
    blocks = 2 * _nbytes((tr, d), F32) + _nbytes((tr, d), BF16)
    return pl.pallas_call(
        _layer_norm_kernel,
        grid=(m // tr,),
        in_specs=[
            pl.BlockSpec((tr, d), lambda i: (i, 0)),
            pl.BlockSpec((1, d), lambda i: (0, 0)),
            pl.BlockSpec((1, d), lambda i: (0, 0)),
        ],
        out_specs=[
            pl.BlockSpec((tr, d), lambda i: (i, 0)),
            pl.BlockSpec((tr, d), lambda i: (i, 0)),
        ],
        out_shape=[jax.ShapeDtypeStruct((m, d), F32), jax.ShapeDtypeStruct((m, d), BF16)],
        compiler_params=_params(("arbitrary",), blocks, temp_bytes=2 * _nbytes((tr, d), F32)),
        name=name,
    )(x, g.reshape(1, d), b.reshape(1, d))


def _merge_kernel(*refs, n_branch, w_mix, n_side):
    y_ref, zg_ref, wb_ref, wg_ref, bg_ref = refs[:5]
    side_in = refs[5:5 + n_side]
    o_ref = refs[5 + n_side]
    side_out = refs[6 + n_side:]
    zg = zg_ref[...]
    acc = None
    for i in range(n_branch):
        gate = jax.nn.sigmoid(jnp.dot(zg, wg_ref[i], preferred_element_type=F32) + bg_ref[i])
        br = jnp.dot(y_ref[:, i * w_mix:(i + 1) * w_mix], wb_ref[i], preferred_element_type=F32)
        acc = gate * br if acc is None else acc + gate * br
    o_ref[...] = acc.astype(o_ref.dtype)
    _run_side_casts(side_in, side_out)


def _merge(y, zg, w_branch, w_gate, b_gate, *, tm=1024, tn=512, casts=(), name):
    m = y.shape[0]
    n_branch, w_mix, d = w_branch.shape
    rank = w_gate.shape[1]
    tm = _tile(m, tm, V7X_BF16_ROWS)
    tn = _tile(d, tn, V7X_MXU_DIM)
    grid = (m // tm, d // tn)
    side = _plan_side_casts(casts, grid)
    blocks = (_nbytes((tm, n_branch * w_mix), BF16) + _nbytes((tm, rank), BF16)
              + _nbytes((n_branch, w_mix, tn), BF16) + _nbytes((n_branch, rank, tn), BF16)
              + _nbytes((n_branch, 1, tn), F32) + _nbytes((tm, tn), BF16) + side.block_bytes)
    return pl.pallas_call(
        functools.partial(_merge_kernel, n_branch=n_branch, w_mix=w_mix, n_side=len(casts)),
        grid=grid,
        in_specs=[
            pl.BlockSpec((tm, n_branch * w_mix), lambda i, j: (i, 0)),
            pl.BlockSpec((tm, rank), lambda i, j: (i, 0)),
            pl.BlockSpec((n_branch, w_mix, tn), lambda i, j: (0, 0, j)),
            pl.BlockSpec((n_branch, rank, tn), lambda i, j: (0, 0, j)),
            pl.BlockSpec((n_branch, 1, tn), lambda i, j: (0, 0, j)),
        ] + side.in_specs,
        out_specs=[pl.BlockSpec((tm, tn), lambda i, j: (i, j))] + side.out_specs,
        out_shape=[jax.ShapeDtypeStruct((m, d), BF16)] + side.out_shapes,
        compiler_params=_params(("arbitrary", "arbitrary"), blocks, temp_bytes=6 * _nbytes((tm, tn), F32)),
        name=name,
    )(y, zg, w_branch, w_gate, b_gate.reshape(n_branch, 1, d), *side.args)


def _mixers_kernel(val_ref, gate_ref, bg_ref, cg_ref, hh_ref, pc_ref, u_ref, v_ref, zg_ref,
                   valh_ref, gateh_ref, cgh_ref, hhh_ref, pch_ref,
                   caw_ref, cab_ref, lag_ref, lab_ref, cbw_ref, pw_ref, ps_ref,
                   ldg_ref, ldb_ref, sw_ref, sbt_ref,
                   y_ref, zgb_ref,
                   hext_ref, hsh_ref, gext_ref, pa_ref, pb_ref, conv_ref,
                   *, ts, seq_tiles, w_mix, halo, halo_b):
    i = pl.program_id(0)
    seq_tile = i % seq_tiles
    first = seq_tile == 0

    ka = caw_ref.shape[0]
    hext_ref[0:halo, :] = jnp.where(first, 0.0, valh_ref[...] * jax.nn.sigmoid(gateh_ref[...]))
    hext_ref[halo:halo + ts, :] = val_ref[...] * jax.nn.sigmoid(gate_ref[...])
    sh_rows = hsh_ref.shape[1]
    for s in range(1, V7X_SUBLANES):
        hsh_ref[s - 1] = hext_ref[s:s + sh_rows, :]
    rc = min(CONV_ROW_CHUNK, ts)
    for c0 in range(0, w_mix, V7X_LANES):
        cols = slice(c0, c0 + V7X_LANES)
        wk = caw_ref[:, cols]
        bias = cab_ref[:, cols]
        for r0 in range(0, ts, rc):
            acc = jnp.broadcast_to(bias, (rc, V7X_LANES))
            for k in range(ka):
                off = halo - (ka - 1) + k
                s, base = off % V7X_SUBLANES, r0 + off - off % V7X_SUBLANES
                if s == 0:
                    win = hext_ref[base:base + rc, cols]
                else:
                    win = hsh_ref[s - 1, base:base + rc, cols]
                acc = acc + wk[k:k + 1, :] * win
            conv_ref[r0:r0 + rc, cols] = acc
    ya = _ln(conv_ref[...], lag_ref[...], lab_ref[...])
    y_ref[:, 0:w_mix] = (ya * jax.nn.sigmoid(ya)).astype(BF16)

    kb = cbw_ref.shape[0]
    gext_ref[0:halo_b, :] = jnp.where(first, 0.0, cgh_ref[...] * hhh_ref[...])
    gext_ref[halo_b:halo_b + ts, :] = cg_ref[...] * hh_ref[...]
    cb = None
    for k in range(kb):
        start = halo_b - (kb - 1) + k
        term = cbw_ref[k:k + 1, :] * gext_ref[start:start + ts, :]
        cb = term if cb is None else cb + term
    y_ref[:, w_mix:2 * w_mix] = (bg_ref[...] * cb).astype(BF16)

    n_groups = pw_ref.shape[0]
    pg = w_mix // n_groups
    ext = halo + ts
    pa_ref[0:halo, :] = jnp.where(first, 0.0, pch_ref[...])
    pa_ref[halo:ext, :] = pc_ref[...]
    pos = (lax.broadcasted_iota(jnp.int32, (ts, pg), 0) + (seq_tile * ts + 1)).astype(F32)
    for g, win in enumerate(POOL_WINDOWS):
        cols = slice(g * pg, (g + 1) * pg)
        src, dst = pa_ref, pb_ref
        span, lo = 1, V7X_SUBLANES
        while span < win:
            dst[lo:ext, cols] = src[lo:ext, cols] + src[lo - span:ext - span, cols]
            src, dst = dst, src
            span, lo = 2 * span, lo + V7X_SUBLANES
        mean = src[halo:ext, cols] / jnp.minimum(pos, float(win))
        pooled = (mean - pc_ref[:, cols]).astype(BF16)
        out = jnp.dot(pooled, pw_ref[g], preferred_element_type=F32)
        y_ref[:, 2 * w_mix + g * pg:2 * w_mix + (g + 1) * pg] = (out * ps_ref[:, cols]).astype(BF16)

    chunk = sw_ref.shape[1]
    sg = w_mix // sw_ref.shape[0]
    uu = jax.nn.gelu(u_ref[...])
    vn = _ln(jax.nn.gelu(v_ref[...]), ldg_ref[...], ldb_ref[...]).astype(BF16)
    causal = (lax.broadcasted_iota(jnp.int32, (chunk, chunk), 0)
              >= lax.broadcasted_iota(jnp.int32, (chunk, chunk), 1))
    for g in range(sw_ref.shape[0]):
        wt = jnp.where(causal, sw_ref[g], 0.0).astype(BF16)
        bias = sbt_ref[:, g:g + 1]
        for c in range(ts // chunk):
            rows = slice(c * chunk, (c + 1) * chunk)
            cols = slice(g * sg, (g + 1) * sg)
            sp = jnp.dot(wt, vn[rows, cols], preferred_element_type=F32) + bias
            y_ref[rows, 3 * w_mix + g * sg:3 * w_mix + (g + 1) * sg] = (uu[rows, cols] * sp).astype(BF16)

    zgb_ref[...] = zg_ref[...].astype(BF16)


def _mixers(z, seq, conv_a_w, conv_a_b, ln_a_g, ln_a_b, conv_b_w, pool_w, pool_scale,
            ln_d_g, ln_d_b, sgu_w, sgu_b, rank, *, ts=256, name):
    m = z.shape[0]
    ka, w_mix = conv_a_w.shape
    kb = conv_b_w.shape[0]
    chunk = sgu_w.shape[-1]
    pool_levels = max(POOL_WINDOWS).bit_length() - 1
    halo = _round_up(max(ka - 1, V7X_SUBLANES * pool_levels), V7X_SUBLANES)
    halo_b = _round_up(kb - 1, V7X_SUBLANES)
    ts = _tile(seq, ts, max(chunk, halo))
    assert ts % chunk == 0 and ts % halo == 0 and ts % halo_b == 0
    assert (8 * w_mix) % rank == 0
    seq_tiles = seq // ts

    def col(c):
        return pl.BlockSpec((ts, w_mix), lambda i: (i, c))

    def halo_spec(rows, c):
        per = ts // rows
        return pl.BlockSpec((rows, w_mix), lambda i: (jnp.maximum(i * per - 1, 0), c))

    def whole(shape):
        return pl.BlockSpec(shape, lambda i: (0,) * len(shape))

    row = lambda a: a.reshape(1, w_mix)
    params = (conv_a_w, row(conv_a_b), row(ln_a_g), row(ln_a_b), conv_b_w, pool_w.astype(BF16),
              row(pool_scale), row(ln_d_g), row(ln_d_b), sgu_w, sgu_b.T)
    in_specs = ([col(c) for c in range(8)]
                + [pl.BlockSpec((ts, rank), lambda i: (i, 8 * w_mix // rank))]
                + [halo_spec(halo, 0), halo_spec(halo, 1), halo_spec(halo_b, 3), halo_spec(halo_b, 4),
                   halo_spec(halo, 5)]
                + [whole(p.shape) for p in params])
    blocks = (8 * _nbytes((ts, w_mix), F32) + _nbytes((ts, rank), F32)
              + 5 * _nbytes((halo, w_mix), F32) + sum(_nbytes(p.shape, p.dtype) for p in params)
              + _nbytes((ts, 4 * w_mix), BF16) + _nbytes((ts, rank), BF16))
    sh_rows = halo + ts - V7X_SUBLANES
    scratch = [
        pltpu.VMEM((halo + ts, w_mix), F32),
        pltpu.VMEM((V7X_SUBLANES - 1, sh_rows, w_mix), F32),
        pltpu.VMEM((halo_b + ts, w_mix), F32),
        pltpu.VMEM((halo + ts, w_mix), F32),
        pltpu.VMEM((halo + ts, w_mix), F32),
        pltpu.VMEM((ts, w_mix), F32),
    ]
    scratch_bytes = (5 + V7X_SUBLANES - 1) * _nbytes((halo + ts, w_mix), F32)
    return pl.pallas_call(
        functools.partial(_mixers_kernel, ts=ts, seq_tiles=seq_tiles, w_mix=w_mix, halo=halo, halo_b=halo_b),
        grid=(m // ts,),
        in_specs=in_specs,
        out_specs=[
            pl.BlockSpec((ts, 4 * w_mix), lambda i: (i, 0)),
            pl.BlockSpec((ts, rank), lambda i: (i, 0)),
        ],
        out_shape=[jax.ShapeDtypeStruct((m, 4 * w_mix), BF16), jax.ShapeDtypeStruct((m, rank), BF16)],
        scratch_shapes=scratch,
        compiler_params=_params(("arbitrary",), blocks, scratch_bytes=scratch_bytes,
                                temp_bytes=8 * _nbytes((ts, w_mix), F32)),
        name=name,
    )(*([z] * 14), *params)


def kernel(x, w_in, conv_a_w, conv_a_b, ln_a_g, ln_a_b, conv_b_w, pool_w, pool_scale, ln_d_g, ln_d_b,
           sgu_w, sgu_b, w_branch, w_gate, b_gate, w_o, ln1_g, ln1_b, w_up, w_down, ln2_g, ln2_b):
    batch, seq, d = x.shape
    depth = w_in.shape[0]
    n_branch, w_mix = w_branch.shape[1], w_branch.shape[2]
    rank = w_gate.shape[2]
    alpha = float((2 * depth) ** 0.25)
    m = batch * seq

    xf = x.reshape(m, d)
    xb = xf.astype(BF16)
    w_branch_rows = w_branch.reshape(depth, n_branch * w_mix, d)
    w_gate_rows = w_gate.reshape(depth, n_branch * rank, d)
    w_in_b = w_in[0].astype(BF16)

    for l in range(depth):
        z, w_branch_b, w_gate_b, w_o_b = _matmul(
            xb, w_in_b, F32, tn=512, name=f"in_proj_{l}",
            casts=(SideCast(w_branch_rows, l), SideCast(w_gate_rows, l), SideCast(w_o, l)))
        y, zg = _mixers(z, seq, conv_a_w[l], conv_a_b[l], ln_a_g[l], ln_a_b[l], conv_b_w[l], pool_w[l],
                        pool_scale[l], ln_d_g[l], ln_d_b[l], sgu_w[l], sgu_b[l], rank, name=f"mixers_{l}")
        merged, w_up_b = _merge(
            y, zg, w_branch_b.reshape(n_branch, w_mix, d), w_gate_b.reshape(n_branch, rank, d), b_gate[l],
            name=f"merge_{l}", casts=(SideCast(w_up, l),))
        (pre,) = _matmul_residual(merged, w_o_b, xf, alpha, name=f"out_proj_{l}")
        xf, xb = _layer_norm(pre, ln1_g[l], ln1_b[l], name=f"ln1_{l}")
        h, w_down_b = _matmul(xb, w_up_b, BF16, relu2=True, name=f"ffn_up_{l}",
                              casts=(SideCast(w_down, l),))
        next_casts = (SideCast(w_in, l + 1),) if l + 1 < depth else ()
        pre, *next_w = _matmul_residual(h, w_down_b, xf, alpha, name=f"ffn_down_{l}", casts=next_casts)
        if next_w:
            w_in_b = next_w[0]
        xf, xb = _layer_norm(pre, ln2_g[l], ln2_b[l], name=f"ln2_{l}")
    return xf.reshape(batch, seq, d)
```

```python
import functools
from typing import NamedTuple

import jax
import jax.numpy as jnp
from jax import lax
from jax.experimental import pallas as pl
from jax.experimental.pallas import tpu as pltpu

F32 = jnp.float32
BF16 = jnp.bfloat16

LN_EPS = 1e-5
POOL_WINDOWS = (2, 4, 8, 16)

V7X_VMEM_BYTES = 64 * 1024 * 1024
V7X_LANES = 128
V7X_SUBLANES = 8
V7X_BF16_ROWS = 16
V7X_MXU_DIM = 256

CONV_ROW_CHUNK = 64


def _round_up(n, m):
    return (n + m - 1) // m * m


def _tile(dim, pref, align):
    if dim <= pref:
        return dim
    t = pref - pref % align
    while t >= align:
        if dim % t == 0:
            return t
        t -= align
    return dim


def _nbytes(shape, dtype):
    n = 1
    for s in shape:
        n *= s
    return n * jnp.dtype(dtype).itemsize


def _params(semantics, block_bytes, scratch_bytes=0, temp_bytes=0):
    need = 2 * block_bytes + scratch_bytes + temp_bytes
    limit = min(max(need, 16 * 1024 * 1024), V7X_VMEM_BYTES - 6 * 1024 * 1024)
    return pltpu.CompilerParams(dimension_semantics=semantics, vmem_limit_bytes=limit)


class SideCast(NamedTuple):
    stacked: jax.Array
    layer: int


class _SidePlan(NamedTuple):
    in_specs: list
    out_specs: list
    out_shapes: list
    args: list
    block_bytes: int


def _plan_side_casts(casts, grid):
    steps = 1
    for g in grid:
        steps *= g

    def linear_step(*idx):
        s = idx[0]
        for g, i in zip(grid[1:], idx[1:]):
            s = s * g + i
        return s

    plan = _SidePlan([], [], [], [], 0)
    block_bytes = 0
    for cast in casts:
        _, rows, cols = cast.stacked.shape
        rb = rows
        for cand in range(V7X_BF16_ROWS, rows + 1, V7X_BF16_ROWS):
            if rows % cand == 0 and rows // cand <= steps:
                rb = cand
                break
        nb = rows // rb
        layer = cast.layer
        plan.in_specs.append(pl.BlockSpec(
            (None, rb, cols), lambda *idx, nb=nb, layer=layer: (layer, jnp.minimum(linear_step(*idx), nb - 1), 0)))
        plan.out_specs.append(pl.BlockSpec(
            (rb, cols), lambda *idx, nb=nb: (jnp.minimum(linear_step(*idx), nb - 1), 0)))
        plan.out_shapes.append(jax.ShapeDtypeStruct((rows, cols), BF16))
        plan.args.append(cast.stacked)
        block_bytes += _nbytes((rb, cols), F32) + _nbytes((rb, cols), BF16)
    return plan._replace(block_bytes=block_bytes)


def _run_side_casts(side_in_refs, side_out_refs):
    for src, dst in zip(side_in_refs, side_out_refs):
        dst[...] = src[...].astype(BF16)


def _ln_matmul_kernel(*refs, relu2, normalize, n_side, n_chunks):
    if normalize:
        pre_ref, g_ref, b_ref, w_ref = refs[:4]
        n_in = 4
    else:
        pre_ref, w_ref = refs[:2]
        n_in = 2
    side_in = refs[n_in:n_in + n_side]
    o_ref = refs[n_in + n_side]
    n_out = 2 if normalize else 1
    xf_ref = refs[n_in + n_side + 1] if normalize else None
    side_out = refs[n_in + n_side + n_out:n_in + 2 * n_side + n_out]
    xbuf_ref = refs[-1]
    i, j = pl.program_id(0), pl.program_id(1)
    last = pl.num_programs(0) - 1
    rb = pre_ref.shape[0]

    def prepare_rows():
        v = pre_ref[...]
        if normalize:
            v = _ln(v, g_ref[...], b_ref[...])
            xf_ref[...] = v
        row0 = pl.multiple_of(jnp.minimum(j, n_chunks - 1) * rb, rb)
        xbuf_ref[i % 2, pl.ds(row0, rb), :] = v.astype(BF16)

    def multiply():
        acc = jnp.dot(xbuf_ref[(i + 1) % 2], w_ref[...], preferred_element_type=F32)
        if relu2:
            r = jnp.maximum(acc, 0.0)
            acc = r * r
        o_ref[...] = acc.astype(o_ref.dtype)

    @pl.when(i == 0)
    def _():
        prepare_rows()

    @pl.when(jnp.logical_and(i > 0, i < last))
    def _():
        multiply()
        prepare_rows()

    @pl.when(i == last)
    def _():
        multiply()

    _run_side_casts(side_in, side_out)


def _ln_matmul(pre, ln_g, ln_b, w, out_dtype, *, relu2=False, tm=1024, tn=1024, casts=(), name):
    m, k = pre.shape
    n = w.shape[1]
    normalize = ln_g is not None
    tm = _tile(m, tm, V7X_BF16_ROWS)
    tn = _tile(n, tn, V7X_MXU_DIM)
    n_i, n_j = m // tm, n // tn
    n_chunks = 1
    while n_chunks * 2 <= n_j and tm % (n_chunks * 2 * V7X_BF16_ROWS) == 0:
        n_chunks *= 2
    rb = tm // n_chunks
    grid = (n_i + 1, n_j)
    side = _plan_side_casts(casts, grid)

    def chunk_index(i, j):
        blk = i * n_chunks + jnp.minimum(j, n_chunks - 1)
        return (jnp.where(i < n_i, blk, n_i * n_chunks - 1), 0)

    in_specs = [pl.BlockSpec((rb, k), chunk_index)]
    args = [pre]
    if normalize:
        in_specs += [pl.BlockSpec((1, k), lambda i, j: (0, 0))] * 2
        args += [ln_g.reshape(1, k), ln_b.reshape(1, k)]
    in_specs.append(pl.BlockSpec((k, tn), lambda i, j: (0, jnp.where(i == 0, 0, j))))
    args.append(w)
    out_specs = [pl.BlockSpec((tm, tn), lambda i, j: (jnp.maximum(i - 1, 0), jnp.where(i == 0, 0, j)))]
    out_shape = [jax.ShapeDtypeStruct((m, n), out_dtype)]
    if normalize:
        out_specs.append(pl.BlockSpec((rb, k), chunk_index))
        out_shape.append(jax.ShapeDtypeStruct((m, k), F32))
    blocks = (2 * _nbytes((rb, k), F32) + _nbytes((k, tn), w.dtype) + _nbytes((tm, tn), out_dtype)
              + side.block_bytes)
    return pl.pallas_call(
        functools.partial(_ln_matmul_kernel, relu2=relu2, normalize=normalize, n_side=len(casts),
                          n_chunks=n_chunks),
        grid=grid,
        in_specs=in_specs + side.in_specs,
        out_specs=out_specs + side.out_specs,
        out_shape=out_shape + side.out_shapes,
        scratch_shapes=[pltpu.VMEM((2, tm, k), BF16)],
        compiler_params=_params(("arbitrary", "arbitrary"), blocks, scratch_bytes=_nbytes((2, tm, k), BF16),
                                temp_bytes=2 * _nbytes((tm, tn), F32) + 4 * _nbytes((rb, k), F32)),
        name=name,
    )(*args, *side.args)


def _matmul_residual_kernel(*refs, alpha, n_side):
    x_ref, w_ref, r_ref = refs[:3]
    side_in = refs[3:3 + n_side]
    o_ref = refs[3 + n_side]
    side_out = refs[4 + n_side:]

    @pl.when(pl.program_id(2) == 0)
    def _():
        o_ref[...] = alpha * r_ref[...]

    o_ref[...] += jnp.dot(x_ref[...], w_ref[...], preferred_element_type=F32)
    _run_side_casts(side_in, side_out)


def _matmul_residual(x, w, resid, alpha, *, tm=1024, tn=1024, tk=4096, casts=(), name):
    m, k = x.shape
    n = w.shape[1]
    tm = _tile(m, tm, V7X_BF16_ROWS)
    tn = _tile(n, tn, V7X_MXU_DIM)
    tk = _tile(k, tk, V7X_MXU_DIM)
    grid = (m // tm, n // tn, k // tk)
    side = _plan_side_casts(casts, grid)
    blocks = (_nbytes((tm, tk), x.dtype) + _nbytes((tk, tn), w.dtype)
              + 2 * _nbytes((tm, tn), F32) + side.block_bytes)
    return pl.pallas_call(
        functools.partial(_matmul_residual_kernel, alpha=alpha, n_side=len(casts)),
        grid=grid,
        in_specs=[
            pl.BlockSpec((tm, tk), lambda i, j, kk: (i, kk)),
            pl.BlockSpec((tk, tn), lambda i, j, kk: (kk, j)),
            pl.BlockSpec((tm, tn), lambda i, j, kk: (i, j)),
        ] + side.in_specs,
        out_specs=[pl.BlockSpec((tm, tn), lambda i, j, kk: (i, j))] + side.out_specs,
        out_shape=[jax.ShapeDtypeStruct((m, n), F32)] + side.out_shapes,
        compiler_params=_params(("arbitrary", "arbitrary", "arbitrary"), blocks,
                                temp_bytes=2 * _nbytes((tm, tn), F32)),
        name=name,
    )(x, w, resid, *side.args)


def _ln(v, g, b):
    mu = jnp.mean(v, axis=-1, keepdims=True)
    d = v - mu
    var = jnp.mean(d * d, axis=-1, keepdims=True)
    return d * lax.rsqrt(var + LN_EPS) * g + b


def _layer_norm_kernel(x_ref, g_ref, b_ref, o_ref):
    o_ref[...] = _ln(x_ref[...], g_ref[...], b_ref[...])


def _layer_norm(x, g, b, *, tr=256, name):
    m, d = x.shape
    tr = _tile(m, tr, V7X_SUBLANES)
    blocks = 2 * _nbytes((tr, d), F32)
    return pl.pallas_call(
        _layer_norm_kernel,
        grid=(m // tr,),
        in_specs=[
            pl.BlockSpec((tr, d), lambda i: (i, 0)),
            pl.BlockSpec((1, d), lambda i: (0, 0)),
            pl.BlockSpec((1, d), lambda i: (0, 0)),
        ],
        out_specs=pl.BlockSpec((tr, d), lambda i: (i, 0)),
        out_shape=jax.ShapeDtypeStruct((m, d), F32),
        compiler_params=_params(("arbitrary",), blocks, temp_bytes=2 * _nbytes((tr, d), F32)),
        name=name,
    )(x, g.reshape(1, d), b.reshape(1, d))


def _merge_kernel(*refs, n_branch, w_mix, n_side):
    y_ref, zg_ref, wb_ref, wg_ref, bg_ref = refs[:5]
    side_in = refs[5:5 + n_side]
    o_ref = refs[5 + n_side]
    side_out = refs[6 + n_side:]
    zg = zg_ref[...]
    acc = None
    for i in range(n_branch):
        gate = jax.nn.sigmoid(jnp.dot(zg, wg_ref[i], preferred_element_type=F32) + bg_ref[i])
        br = jnp.dot(y_ref[:, i * w_mix:(i + 1) * w_mix], wb_ref[i], preferred_element_type=F32)
        acc = gate * br if acc is None else acc + gate * br
    o_ref[...] = acc.astype(o_ref.dtype)
    _run_side_casts(side_in, side_out)


def _merge(y, zg, w_branch, w_gate, b_gate, *, tm=1024, tn=512, casts=(), name):
    m = y.shape[0]
    n_branch, w_mix, d = w_branch.shape
    rank = w_gate.shape[1]
    tm = _tile(m, tm, V7X_BF16_ROWS)
    tn = _tile(d, tn, V7X_MXU_DIM)
    grid = (m // tm, d // tn)
    side = _plan_side_casts(casts, grid)
    blocks = (_nbytes((tm, n_branch * w_mix), BF16) + _nbytes((tm, rank), BF16)
              + _nbytes((n_branch, w_mix, tn), BF16) + _nbytes((n_branch, rank, tn), BF16)
              + _nbytes((n_branch, 1, tn), F32) + _nbytes((tm, tn), BF16) + side.block_bytes)
    return pl.pallas_call(
        functools.partial(_merge_kernel, n_branch=n_branch, w_mix=w_mix, n_side=len(casts)),
        grid=grid,
        in_specs=[
            pl.BlockSpec((tm, n_branch * w_mix), lambda i, j: (i, 0)),
            pl.BlockSpec((tm, rank), lambda i, j: (i, 0)),
            pl.BlockSpec((n_branch, w_mix, tn), lambda i, j: (0, 0, j)),
            pl.BlockSpec((n_branch, rank, tn), lambda i, j: (0, 0, j)),
            pl.BlockSpec((n_branch, 1, tn), lambda i, j: (0, 0, j)),
        ] + side.in_specs,
        out_specs=[pl.BlockSpec((tm, tn), lambda i, j: (i, j))] + side.out_specs,
        out_shape=[jax.ShapeDtypeStruct((m, d), BF16)] + side.out_shapes,
        compiler_params=_params(("arbitrary", "arbitrary"), blocks, temp_bytes=6 * _nbytes((tm, tn), F32)),
        name=name,
    )(y, zg, w_branch, w_gate, b_gate.reshape(n_branch, 1, d), *side.args)


def _mixers_kernel(val_ref, gate_ref, bg_ref, cg_ref, hh_ref, pc_ref, u_ref, v_ref, zg_ref,
                   valh_ref, gateh_ref, cgh_ref, hhh_ref, pch_ref,
                   caw_ref, cab_ref, lag_ref, lab_ref, cbw_ref, pw_ref, ps_ref,
                   ldg_ref, ldb_ref, sw_ref, sbt_ref,
                   y_ref, zgb_ref,
                   hext_ref, hsh_ref, gext_ref, pa_ref, pb_ref, conv_ref,
                   *, ts, seq_tiles, w_mix, halo, halo_b):
    i = pl.program_id(0)
    seq_tile = i % seq_tiles
    first = seq_tile == 0

    ka = caw_ref.shape[0]
    hext_ref[0:halo, :] = jnp.where(first, 0.0, valh_ref[...] * jax.nn.sigmoid(gateh_ref[...]))
    hext_ref[halo:halo + ts, :] = val_ref[...] * jax.nn.sigmoid(gate_ref[...])
    sh_rows = hsh_ref.shape[1]
    for s in range(1, V7X_SUBLANES):
        hsh_ref[s - 1] = hext_ref[s:s + sh_rows, :]
    rc = min(CONV_ROW_CHUNK, ts)
    for c0 in range(0, w_mix, V7X_LANES):
        cols = slice(c0, c0 + V7X_LANES)
        wk = caw_ref[:, cols]
        bias = cab_ref[:, cols]
        for r0 in range(0, ts, rc):
            acc = jnp.broadcast_to(bias, (rc, V7X_LANES))
            for k in range(ka):
                off = halo - (ka - 1) + k
                s, base = off % V7X_SUBLANES, r0 + off - off % V7X_SUBLANES
                if s == 0:
                    win = hext_ref[base:base + rc, cols]
                else:
                    win = hsh_ref[s - 1, base:base + rc, cols]
                acc = acc + wk[k:k + 1, :] * win
            conv_ref[r0:r0 + rc, cols] = acc
    ya = _ln(conv_ref[...], lag_ref[...], lab_ref[...])
    y_ref[:, 0:w_mix] = (ya * jax.nn.sigmoid(ya)).astype(BF16)

    kb = cbw_ref.shape[0]
    gext_ref[0:halo_b, :] = jnp.where(first, 0.0, cgh_ref[...] * hhh_ref[...])
    gext_ref[halo_b:halo_b + ts, :] = cg_ref[...] * hh_ref[...]
    cb = None
    for k in range(kb):
        start = halo_b - (kb - 1) + k
        term = cbw_ref[k:k + 1, :] * gext_ref[start:start + ts, :]
        cb = term if cb is None else cb + term
    y_ref[:, w_mix:2 * w_mix] = (bg_ref[...] * cb).astype(BF16)

    n_groups = pw_ref.shape[0]
    pg = w_mix // n_groups
    ext = halo + ts
    pa_ref[0:halo, :] = jnp.where(first, 0.0, pch_ref[...])
    pa_ref[halo:ext, :] = pc_ref[...]
    pos = (lax.broadcasted_iota(jnp.int32, (ts, pg), 0) + (seq_tile * ts + 1)).astype(F32)
    for g, win in enumerate(POOL_WINDOWS):
        cols = slice(g * pg, (g + 1) * pg)
        src, dst = pa_ref, pb_ref
        span, lo = 1, V7X_SUBLANES
        while span < win:
            dst[lo:ext, cols] = src[lo:ext, cols] + src[lo - span:ext - span, cols]
            src, dst = dst, src
            span, lo = 2 * span, lo + V7X_SUBLANES
        mean = src[halo:ext, cols] / jnp.minimum(pos, float(win))
        pooled = (mean - pc_ref[:, cols]).astype(BF16)
        out = jnp.dot(pooled, pw_ref[g], preferred_element_type=F32)
        y_ref[:, 2 * w_mix + g * pg:2 * w_mix + (g + 1) * pg] = (out * ps_ref[:, cols]).astype(BF16)

    chunk = sw_ref.shape[1]
    sg = w_mix // sw_ref.shape[0]
    uu = jax.nn.gelu(u_ref[...])
    vn = _ln(jax.nn.gelu(v_ref[...]), ldg_ref[...], ldb_ref[...]).astype(BF16)
    causal = (lax.broadcasted_iota(jnp.int32, (chunk, chunk), 0)
              >= lax.broadcasted_iota(jnp.int32, (chunk, chunk), 1))
    for g in range(sw_ref.shape[0]):
        wt = jnp.where(causal, sw_ref[g], 0.0).astype(BF16)
        bias = sbt_ref[:, g:g + 1]
        for c in range(ts // chunk):
            rows = slice(c * chunk, (c + 1) * chunk)
            cols = slice(g * sg, (g + 1) * sg)
            sp = jnp.dot(wt, vn[rows, cols], preferred_element_type=F32) + bias
            y_ref[rows, 3 * w_mix + g * sg:3 * w_mix + (g + 1) * sg] = (uu[rows, cols] * sp).astype(BF16)

    zgb_ref[...] = zg_ref[...].astype(BF16)


def _mixers(z, seq, conv_a_w, conv_a_b, ln_a_g, ln_a_b, conv_b_w, pool_w, pool_scale,
            ln_d_g, ln_d_b, sgu_w, sgu_b, rank, *, ts=256, name):
    m = z.shape[0]
    ka, w_mix = conv_a_w.shape
    kb = conv_b_w.shape[0]
    chunk = sgu_w.shape[-1]
    pool_levels = max(POOL_WINDOWS).bit_length() - 1
    halo = _round_up(max(ka - 1, V7X_SUBLANES * pool_levels), V7X_SUBLANES)
    halo_b = _round_up(kb - 1, V7X_SUBLANES)
    ts = _tile(seq, ts, max(chunk, halo))
    assert ts % chunk == 0 and ts % halo == 0 and ts % halo_b == 0
    assert (8 * w_mix) % rank == 0
    seq_tiles = seq // ts

    def col(c):
        return pl.BlockSpec((ts, w_mix), lambda i: (i, c))

    def halo_spec(rows, c):
        per = ts // rows
        return pl.BlockSpec((rows, w_mix), lambda i: (jnp.maximum(i * per - 1, 0), c))

    def whole(shape):
        return pl.BlockSpec(shape, lambda i: (0,) * len(shape))

    row = lambda a: a.reshape(1, w_mix)
    params = (conv_a_w, row(conv_a_b), row(ln_a_g), row(ln_a_b), conv_b_w, pool_w.astype(BF16),
              row(pool_scale), row(ln_d_g), row(ln_d_b), sgu_w, sgu_b.T)
    in_specs = ([col(c) for c in range(8)]
                + [pl.BlockSpec((ts, rank), lambda i: (i, 8 * w_mix // rank))]
                + [halo_spec(halo, 0), halo_spec(halo, 1), halo_spec(halo_b, 3), halo_spec(halo_b, 4),
                   halo_spec(halo, 5)]
                + [whole(p.shape) for p in params])
    blocks = (8 * _nbytes((ts, w_mix), F32) + _nbytes((ts, rank), F32)
              + 5 * _nbytes((halo, w_mix), F32) + sum(_nbytes(p.shape, p.dtype) for p in params)
              + _nbytes((ts, 4 * w_mix), BF16) + _nbytes((ts, rank), BF16))
    sh_rows = halo + ts - V7X_SUBLANES
    scratch = [
        pltpu.VMEM((halo + ts, w_mix), F32),
        pltpu.VMEM((V7X_SUBLANES - 1, sh_rows, w_mix), F32),
        pltpu.VMEM((halo_b + ts, w_mix), F32),
        pltpu.VMEM((halo + ts, w_mix), F32),
        pltpu.VMEM((halo + ts, w_mix), F32),
        pltpu.VMEM((ts, w_mix), F32),
    ]
    scratch_bytes = (5 + V7X_SUBLANES - 1) * _nbytes((halo + ts, w_mix), F32)
    return pl.pallas_call(
        functools.partial(_mixers_kernel, ts=ts, seq_tiles=seq_tiles, w_mix=w_mix, halo=halo, halo_b=halo_b),
        grid=(m // ts,),
        in_specs=in_specs,
        out_specs=[
            pl.BlockSpec((ts, 4 * w_mix), lambda i: (i, 0)),
            pl.BlockSpec((ts, rank), lambda i: (i, 0)),
        ],
        out_shape=[jax.ShapeDtypeStruct((m, 4 * w_mix), BF16), jax.ShapeDtypeStruct((m, rank), BF16)],
        scratch_shapes=scratch,
        compiler_params=_params(("arbitrary",), blocks, scratch_bytes=scratch_bytes,
                                temp_bytes=8 * _nbytes((ts, w_mix), F32)),
        name=name,
    )(*([z] * 14), *params)


def kernel(x, w_in, conv_a_w, conv_a_b, ln_a_g, ln_a_b, conv_b_w, pool_w, pool_scale, ln_d_g, ln_d_b,
           sgu_w, sgu_b, w_branch, w_gate, b_gate, w_o, ln1_g, ln1_b, w_up, w_down, ln2_g, ln2_b):
    batch, seq, d = x.shape
    depth = w_in.shape[0]
    n_branch, w_mix = w_branch.shape[1], w_branch.shape[2]
    rank = w_gate.shape[2]
    alpha = float((2 * depth) ** 0.25)
    m = batch * seq

    xf = x.reshape(m, d)
    w_branch_rows = w_branch.reshape(depth, n_branch * w_mix, d)
    w_gate_rows = w_gate.reshape(depth, n_branch * rank, d)
    w_in_b = w_in[0].astype(BF16)

    pre, ln_g, ln_b = xf, None, None
    for l in range(depth):
        z, *rest = _ln_matmul(
            pre, ln_g, ln_b, w_in_b, F32, tn=512, name=f"in_proj_{l}",
            casts=(SideCast(w_branch_rows, l), SideCast(w_gate_rows, l), SideCast(w_o, l)))
        if ln_g is not None:
            xf = rest.pop(0)
        w_branch_b, w_gate_b, w_o_b = rest
        y, zg = _mixers(z, seq, conv_a_w[l], conv_a_b[l], ln_a_g[l], ln_a_b[l], conv_b_w[l], pool_w[l],
                        pool_scale[l], ln_d_g[l], ln_d_b[l], sgu_w[l], sgu_b[l], rank, name=f"mixers_{l}")
        merged, w_up_b = _merge(
            y, zg, w_branch_b.reshape(n_branch, w_mix, d), w_gate_b.reshape(n_branch, rank, d), b_gate[l],
            name=f"merge_{l}", casts=(SideCast(w_up, l),))
        (pre,) = _matmul_residual(merged, w_o_b, xf, alpha, name=f"out_proj_{l}")
        h, xf, w_down_b = _ln_matmul(pre, ln1_g[l], ln1_b[l], w_up_b, BF16, relu2=True, name=f"ffn_up_{l}",
                                     casts=(SideCast(w_down, l),))
        next_casts = (SideCast(w_in, l + 1),) if l + 1 < depth else ()
        pre, *next_w = _matmul_residual(h, w_down_b, xf, alpha, name=f"ffn_down_{l}", casts=next_casts)
        if next_w:
            w_in_b = next_w[0]
        ln_g, ln_b = ln2_g[l], ln2_b[l]
    xf = _layer_norm(pre, ln_g, ln_b, name="ln_final")
    return xf.reshape(batch, seq, d)
```
